```python
import math
import jax, jax.numpy as jnp
from jax import lax
import numpy as np

D_MODEL = 1024
BATCH = 8
SEQ = 4096
DEPTH = 1

HEAD_DIM = 64
SWA_Q_HEADS = 8
SWA_KV_HEADS = 2
SWA_GROUP = SWA_Q_HEADS // SWA_KV_HEADS
WINDOW = 128
SWA_BLOCK = 128
DIFF_HEADS = 4
DIFF_QBLOCK = 128
ROPE_THETA = 10000.0
N_EXPERTS = 32
TOP_K = 4
D_FF = D_MODEL
SWIGLU_LIMIT = 7.0
SWIGLU_ALPHA = 1.702
MOE_BLOCK = 128
RMS_EPS = 1e-5

SWA_Q_W = SWA_Q_HEADS * HEAD_DIM
SWA_KV_W = SWA_KV_HEADS * HEAD_DIM
DIFF_QK_W = DIFF_HEADS * 2 * HEAD_DIM
DIFF_V_W = DIFF_HEADS * 2 * HEAD_DIM
IN_SPLITS = (SWA_Q_W, SWA_KV_W, SWA_KV_W, DIFF_QK_W, DIFF_QK_W, DIFF_V_W, D_MODEL, D_MODEL)
IN_WIDTH = sum(IN_SPLITS)

kernel_name = "hybrid_swa_sink_diffattn_gated_moe"


def rms_norm(x, g):
    xf = x.astype(jnp.float32)
    y = xf * lax.rsqrt(jnp.mean(xf * xf, axis=-1, keepdims=True) + RMS_EPS)
    return (y * g.astype(jnp.float32)).astype(x.dtype)


def rope_tables(seq, dim):
    pos = jnp.arange(seq, dtype=jnp.float32)
    inv_freq = ROPE_THETA ** (-jnp.arange(0, dim, 2, dtype=jnp.float32) / dim)
    ang = pos[:, None] * inv_freq[None, :]
    return jnp.cos(ang), jnp.sin(ang)


def apply_rope(x, cos, sin):
    x1, x2 = jnp.split(x.astype(jnp.float32), 2, axis=-1)
    c = cos[None, :, None, :]
    s = sin[None, :, None, :]
    return jnp.concatenate([x1 * c - x2 * s, x2 * c + x1 * s], axis=-1).astype(x.dtype)


def sliding_window_attention(q, k, v, sinks):
    B, S = q.shape[0], q.shape[1]
    nb = S // SWA_BLOCK
    qb = q.reshape(B, nb, SWA_BLOCK, SWA_KV_HEADS, SWA_GROUP, HEAD_DIM)
    kb = k.reshape(B, nb, SWA_BLOCK, SWA_KV_HEADS, HEAD_DIM)
    vb = v.reshape(B, nb, SWA_BLOCK, SWA_KV_HEADS, HEAD_DIM)

    def with_prev(t):
        prev = jnp.pad(t[:, :-1], ((0, 0), (1, 0), (0, 0), (0, 0), (0, 0)))
        return jnp.concatenate([prev, t], axis=2)

    kc, vc = with_prev(kb), with_prev(vb)
    s = jnp.einsum('bnqhgd,bnkhd->bnhgqk', qb, kc).astype(jnp.float32) * (HEAD_DIM ** -0.5)
    qi = jnp.arange(SWA_BLOCK)[:, None] + SWA_BLOCK
    kj = jnp.arange(2 * SWA_BLOCK)[None, :]
    band = (kj <= qi) & (qi - kj < WINDOW)
    has_prev = (jnp.arange(nb)[:, None, None] > 0) | (kj[None] >= SWA_BLOCK)
    mask = band[None] & has_prev
    s = jnp.where(mask[None, :, None, None], s, -jnp.inf)
    sink = sinks.astype(jnp.float32).reshape(SWA_KV_HEADS, SWA_GROUP)
    sink_b = jnp.broadcast_to(sink[None, None, :, :, None, None], s.shape[:-1] + (1,))
    p = jax.nn.softmax(jnp.concatenate([s, sink_b], axis=-1), axis=-1)[..., :-1]
    o = jnp.einsum('bnhgqk,bnkhd->bnqhgd', p.astype(v.dtype), vc)
    return o.reshape(B, S, SWA_Q_W)


def diff_attention(q, k, v, lam):
    B, S = q.shape[0], q.shape[1]
    nb = S // DIFF_QBLOCK
    kpos = jnp.arange(S)
    scale = HEAD_DIM ** -0.5

    def block(i):
        start = i * DIFF_QBLOCK
        qi = lax.dynamic_slice_in_dim(q, start, DIFF_QBLOCK, axis=1)
        s = jnp.einsum('bqhcd,bkhcd->bhcqk', qi, k).astype(jnp.float32) * scale
        qpos = start + jnp.arange(DIFF_QBLOCK)
        causal = kpos[None, :] <= qpos[:, None]
        p = jax.nn.softmax(jnp.where(causal, s, -jnp.inf), axis=-1)
        a = p[:, :, 0] - lam * p[:, :, 1]
        return jnp.einsum('bhqk,bkhe->bqhe', a.astype(v.dtype), v)

    o = lax.map(block, jnp.arange(nb))
    return o.transpose(1, 0, 2, 3, 4).reshape(B, S, DIFF_HEADS, 2 * HEAD_DIM)


def moe_ffn(x, router_w, router_b, w1, b1, w2, b2):
    B, S, D = x.shape
    N = B * S
    xf = x.reshape(N, D)
    logits = (xf @ router_w + router_b).astype(jnp.float32)
    top_vals, top_idx = lax.top_k(logits, TOP_K)
    gates = jax.nn.softmax(top_vals, axis=-1)
    A = N * TOP_K
    flat_e = top_idx.reshape(A).astype(jnp.int32)
    flat_tok = jnp.arange(A, dtype=jnp.int32) // TOP_K
    flat_w = gates.reshape(A)
    order = jnp.argsort(flat_e, stable=True)
    se, stok, sw = flat_e[order], flat_tok[order], flat_w[order]
    counts = jnp.bincount(flat_e, length=N_EXPERTS).astype(jnp.int32)
    padded = (counts + MOE_BLOCK - 1) // MOE_BLOCK * MOE_BLOCK
    grp_start = jnp.cumsum(counts) - counts
    pad_end = jnp.cumsum(padded)
    pad_start = pad_end - padded
    dest = pad_start[se] + jnp.arange(A, dtype=jnp.int32) - grp_start[se]
    n_rows = (-(-A // MOE_BLOCK) + N_EXPERTS) * MOE_BLOCK
    n_blocks = n_rows // MOE_BLOCK
    row_tok = jnp.zeros((n_rows,), jnp.int32).at[dest].set(stok)
    row_w = jnp.zeros((n_rows,), jnp.float32).at[dest].set(sw)
    block_start = jnp.arange(n_blocks, dtype=jnp.int32) * MOE_BLOCK
    block_e = jnp.minimum(jnp.searchsorted(pad_end, block_start, side='right'), N_EXPERTS - 1)
    xb = xf[row_tok].reshape(n_blocks, MOE_BLOCK, D)

    def expert_block(args):
        xblk, e = args
        h = xblk @ w1[e] + b1[e]
        g, u = jnp.split(h, 2, axis=-1)
        g = jnp.minimum(g, SWIGLU_LIMIT)
        u = jnp.clip(u, -SWIGLU_LIMIT, SWIGLU_LIMIT)
        act = g * jax.nn.sigmoid(SWIGLU_ALPHA * g) * (u + 1.0)
        return act @ w2[e] + b2[e]

    yb = lax.map(expert_block, (xb, block_e)).reshape(n_rows, D)
    y = jnp.zeros((N, D), yb.dtype).at[row_tok].add(yb * row_w[:, None].astype(yb.dtype))
    return y.reshape(B, S, D).astype(x.dtype)


def setup_inputs(seed: int = 0) -> dict:
    key = jax.random.key(seed)
    ks = jax.random.split(key, 22)
    n = lambda k, shape, s: jax.random.normal(k, shape, jnp.float32) * s
    L, D, E, F = DEPTH, D_MODEL, N_EXPERTS, D_FF
    return {
        "x": n(ks[0], (BATCH, SEQ, D), 1.0),
        "attn_norm_g": 1.0 + n(ks[1], (L, D), 0.01),
        "w_in": n(ks[2], (L, D, IN_WIDTH), D ** -0.5),
        "b_in": n(ks[3], (L, IN_WIDTH), 0.02),
        "swa_sinks": n(ks[4], (L, SWA_Q_HEADS), 0.5),
        "diff_lambda_q1": n(ks[5], (L, HEAD_DIM), 0.1),
        "diff_lambda_k1": n(ks[6], (L, HEAD_DIM), 0.1),
        "diff_lambda_q2": n(ks[7], (L, HEAD_DIM), 0.1),
        "diff_lambda_k2": n(ks[8], (L, HEAD_DIM), 0.1),
        "diff_subln_g": 1.0 + n(ks[9], (L, 2 * HEAD_DIM), 0.01),
        "w_swa_out": n(ks[10], (L, SWA_Q_W, D), SWA_Q_W ** -0.5),
        "w_diff_out": n(ks[11], (L, DIFF_V_W, D), DIFF_V_W ** -0.5),
        "w_o": n(ks[12], (L, D, D), D ** -0.5),
        "ffn_norm_g": 1.0 + n(ks[13], (L, D), 0.01),
        "router_w": n(ks[14], (L, D, E), D ** -0.5),
        "router_b": n(ks[15], (L, E), 0.01),
        "w1": n(ks[16], (L, E, D, 2 * F), D ** -0.5),
        "b1": n(ks[17], (L, E, 2 * F), 0.01),
        "w2": n(ks[18], (L, E, F, D), F ** -0.5),
        "b2": n(ks[19], (L, E, D), 0.01),
        "final_norm_g": 1.0 + n(ks[20], (D,), 0.01),
    }


def reference(x, attn_norm_g, w_in, b_in, swa_sinks, diff_lambda_q1, diff_lambda_k1,
              diff_lambda_q2, diff_lambda_k2, diff_subln_g, w_swa_out, w_diff_out, w_o,
              ffn_norm_g, router_w, router_b, w1, b1, w2, b2, final_norm_g):
    B, S = x.shape[0], x.shape[1]
    cos, sin = rope_tables(S, HEAD_DIM)
    split_at = [int(i) for i in np.cumsum(IN_SPLITS)[:-1]]
    f32 = jnp.float32
    for l in range(DEPTH):
        h = rms_norm(x, attn_norm_g[l])
        proj = h @ w_in[l] + b_in[l]
        qa, ka, va, qb, kb, vb, ga, gb = jnp.split(proj, split_at, axis=-1)
        qa = apply_rope(qa.reshape(B, S, SWA_Q_HEADS, HEAD_DIM), cos, sin)
        ka = apply_rope(ka.reshape(B, S, SWA_KV_HEADS, HEAD_DIM), cos, sin)
        va = va.reshape(B, S, SWA_KV_HEADS, HEAD_DIM)
        ya = sliding_window_attention(qa, ka, va, swa_sinks[l]) @ w_swa_out[l]
        qb = apply_rope(qb.reshape(B, S, 2 * DIFF_HEADS, HEAD_DIM), cos, sin).reshape(B, S, DIFF_HEADS, 2, HEAD_DIM)
        kb = apply_rope(kb.reshape(B, S, 2 * DIFF_HEADS, HEAD_DIM), cos, sin).reshape(B, S, DIFF_HEADS, 2, HEAD_DIM)
        vb = vb.reshape(B, S, DIFF_HEADS, 2 * HEAD_DIM)
        lam_init = 0.8 - 0.6 * math.exp(-0.3 * l)
        lam = (jnp.exp(jnp.sum(diff_lambda_q1[l].astype(f32) * diff_lambda_k1[l].astype(f32)))
               - jnp.exp(jnp.sum(diff_lambda_q2[l].astype(f32) * diff_lambda_k2[l].astype(f32)))
               + lam_init)
        ob = rms_norm(diff_attention(qb, kb, vb, lam), diff_subln_g[l]) * (1.0 - lam_init)
        yb = ob.reshape(B, S, DIFF_V_W) @ w_diff_out[l]
        mixed = jax.nn.sigmoid(ga) * ya + jax.nn.sigmoid(gb) * yb
        x = x + mixed @ w_o[l]
        x = x + moe_ffn(rms_norm(x, ffn_norm_g[l]), router_w[l], router_b[l], w1[l], b1[l], w2[l], b2[l])
    return rms_norm(x, final_norm_g)
```

```python
import functools
import math

import jax
import jax.numpy as jnp
import numpy as np
from jax import lax
from jax.experimental import pallas as pl
from jax.experimental.pallas import tpu as pltpu

F32 = jnp.float32
BF16 = jnp.bfloat16
I32 = jnp.int32

HEAD_DIM = 64
SWA_Q_HEADS = 8
SWA_KV_HEADS = 2
SWA_GROUP = SWA_Q_HEADS // SWA_KV_HEADS
WINDOW = 128
DIFF_HEADS = 4
ROPE_THETA = 10000.0
N_EXPERTS = 32
TOP_K = 4
SWIGLU_LIMIT = 7.0
SWIGLU_ALPHA = 1.702
RMS_EPS = 1e-5
LAYER_INDEX = 0
LAM_INIT = 0.8 - 0.6 * math.exp(-0.3 * LAYER_INDEX)

SWA_Q_W = SWA_Q_HEADS * HEAD_DIM
SWA_KV_W = SWA_KV_HEADS * HEAD_DIM
DIFF_W = DIFF_HEADS * 2 * HEAD_DIM
LANES = 128
NEG_BIG = -1e30

VMEM_LIMIT_BYTES = 56 * 1024 * 1024

TM_PROJ = 512
TQ_SWA = 256
TQ_DIFF = 256
TK_DIFF = 256
TM_MIX = 512
TM_DISPATCH = 256
BLK_FFN = 256
TM_COMBINE = 128


def _params(sem, vmem=VMEM_LIMIT_BYTES):
    return pltpu.CompilerParams(dimension_semantics=sem, vmem_limit_bytes=vmem)


def _in_proj_body(x_ref, g_ref, w_ref, b_ref, cos_ref, sin_ref,
                  qa_ref, ka_ref, va_ref, qbt_ref, kb_ref, vbt_ref, ga_ref, gb_ref):
    tm = x_ref.shape[0]
    x = x_ref[...]
    ms = jnp.mean(x * x, axis=-1, keepdims=True)
    h = (x * lax.rsqrt(ms + RMS_EPS) * g_ref[...]).astype(BF16)
    cos = cos_ref[...]
    sin = sin_ref[...]
    lane = lax.broadcasted_iota(I32, (tm, LANES), 1)
    first_half = (lane & (HEAD_DIM // 2)) == 0

    def proj(a, b):
        return jnp.dot(h, w_ref[:, a:b], preferred_element_type=F32) + b_ref[:, a:b]

    def rope_chunk(yc):
        partner = jnp.where(first_half,
                            pltpu.roll(yc, LANES - HEAD_DIM // 2, 1),
                            pltpu.roll(yc, HEAD_DIM // 2, 1))
        return yc * cos + partner * sin

    scale = HEAD_DIM ** -0.5
    o = 0
    y = proj(o, o + SWA_Q_W)
    for c in range(SWA_Q_W // LANES):
        qa_ref[:, c * LANES:(c + 1) * LANES] = (
            rope_chunk(y[:, c * LANES:(c + 1) * LANES]) * scale).astype(BF16)
    o += SWA_Q_W
    y = proj(o, o + 2 * SWA_KV_W)
    ka_ref[...] = rope_chunk(y[:, :SWA_KV_W]).astype(BF16)
    va_ref[...] = y[:, SWA_KV_W:].astype(BF16)
    o += 2 * SWA_KV_W
    y = proj(o, o + DIFF_W)
    nt = tm // TQ_DIFF
    for hd in range(DIFF_HEADS):
        t = (rope_chunk(y[:, hd * LANES:(hd + 1) * LANES]) * scale).T.astype(BF16)
        for tt in range(nt):
            qbt_ref[hd, tt] = t[:, tt * TQ_DIFF:(tt + 1) * TQ_DIFF]
    o += DIFF_W
    y = proj(o, o + DIFF_W)
    for c in range(DIFF_W // LANES):
        kb_ref[:, c * LANES:(c + 1) * LANES] = rope_chunk(
            y[:, c * LANES:(c + 1) * LANES]).astype(BF16)
    o += DIFF_W
    y = proj(o, o + DIFF_W)
    nt = tm // TK_DIFF
    for hd in range(DIFF_HEADS):
        t = y[:, hd * LANES:(hd + 1) * LANES].T.astype(BF16)
        for tt in range(nt):
            vbt_ref[hd, tt] = t[:, tt * TK_DIFF:(tt + 1) * TK_DIFF]
    o += DIFF_W
    d_model = ga_ref.shape[1]
    ga_ref[...] = jax.nn.sigmoid(proj(o, o + d_model)).astype(BF16)
    o += d_model
    gb_ref[...] = jax.nn.sigmoid(proj(o, o + d_model)).astype(BF16)


def _in_proj(x, g, w, b, cos_t, sin_t):
    bsz, seq, d = x.shape
    tm = min(TM_PROJ, seq)
    width = w.shape[1]
    grid = (bsz, seq // tm)
    row = lambda bb, i: (bb, i, 0)
    const = lambda bb, i: (0, 0)
    tile5 = lambda bb, i: (bb, 0, i, 0, 0)
    out_shape = (
        jax.ShapeDtypeStruct((bsz, seq, SWA_Q_W), BF16),
        jax.ShapeDtypeStruct((bsz, seq, SWA_KV_W), BF16),
        jax.ShapeDtypeStruct((bsz, seq, SWA_KV_W), BF16),
        jax.ShapeDtypeStruct((bsz, DIFF_HEADS, seq // TQ_DIFF, LANES, TQ_DIFF), BF16),
        jax.ShapeDtypeStruct((bsz, seq, DIFF_W), BF16),
        jax.ShapeDtypeStruct((bsz, DIFF_HEADS, seq // TK_DIFF, LANES, TK_DIFF), BF16),
        jax.ShapeDtypeStruct((bsz, seq, d), BF16),
        jax.ShapeDtypeStruct((bsz, seq, d), BF16),
    )
    out_specs = (
        pl.BlockSpec((None, tm, SWA_Q_W), row),
        pl.BlockSpec((None, tm, SWA_KV_W), row),
        pl.BlockSpec((None, tm, SWA_KV_W), row),
        pl.BlockSpec((None, DIFF_HEADS, tm // TQ_DIFF, LANES, TQ_DIFF), tile5),
        pl.BlockSpec((None, tm, DIFF_W), row),
        pl.BlockSpec((None, DIFF_HEADS, tm // TK_DIFF, LANES, TK_DIFF), tile5),
        pl.BlockSpec((None, tm, d), row),
        pl.BlockSpec((None, tm, d), row),
    )
    in_specs = [
        pl.BlockSpec((None, tm, d), row),
        pl.BlockSpec((1, d), const),
        pl.BlockSpec((d, width), const),
        pl.BlockSpec((1, width), const),
        pl.BlockSpec((tm, LANES), lambda bb, i: (i, 0)),
        pl.BlockSpec((tm, LANES), lambda bb, i: (i, 0)),
    ]
    return pl.pallas_call(
        _in_proj_body, grid=grid, in_specs=in_specs, out_specs=out_specs,
        out_shape=out_shape, name="in_proj",
        compiler_params=_params(("parallel", "parallel")),
    )(x, g, w, b, cos_t, sin_t)


def _swa_body(sink_ref, q_ref, k_ref, v_ref, o_ref):
    tq = q_ref.shape[0]
    blk = WINDOW
    q0 = pl.program_id(1) * tq
    rows = lax.broadcasted_iota(I32, (blk, 2 * blk), 0)
    cols = lax.broadcasted_iota(I32, (blk, 2 * blk), 1)
    for j in range(tq // blk):
        qstart = q0 + j * blk
        kstart = pl.multiple_of(jnp.maximum(qstart - blk, 0), blk)
        kblk = k_ref[pl.ds(kstart, 2 * blk), :]
        vblk = v_ref[pl.ds(kstart, 2 * blk), :]
        qpos = qstart + rows
        kpos = kstart + cols
        mask = (kpos <= qpos) & (qpos - kpos < WINDOW)
        outs = []
        for hk in range(SWA_KV_HEADS):
            kh = kblk[:, hk * HEAD_DIM:(hk + 1) * HEAD_DIM]
            vh = vblk[:, hk * HEAD_DIM:(hk + 1) * HEAD_DIM]
            for g in range(SWA_GROUP):
                hq = hk * SWA_GROUP + g
                qh = q_ref[j * blk:(j + 1) * blk, hq * HEAD_DIM:(hq + 1) * HEAD_DIM]
                s = lax.dot_general(qh, kh, (((1,), (1,)), ((), ())),
                                    preferred_element_type=F32)
                s = jnp.where(mask, s, NEG_BIG)
                sink = sink_ref[hq]
                m = jnp.maximum(jnp.max(s, axis=-1, keepdims=True), sink)
                p = jnp.exp(s - m)
                denom = jnp.sum(p, axis=-1, keepdims=True) + jnp.exp(sink - m)
                o = jnp.dot(p.astype(BF16), vh, preferred_element_type=F32)
                outs.append(o * (1.0 / denom))
        o_ref[j * blk:(j + 1) * blk, :] = jnp.concatenate(outs, axis=-1).astype(BF16)


def _swa(sinks, qa, ka, va):
    bsz, seq, _ = qa.shape
    tq = min(TQ_SWA, seq)
    grid = (bsz, seq // tq)
    return pl.pallas_call(
        _swa_body, grid=grid,
        in_specs=[
            pl.BlockSpec(memory_space=pltpu.SMEM),
            pl.BlockSpec((None, tq, SWA_Q_W), lambda b, i: (b, i, 0)),
            pl.BlockSpec((None, seq, SWA_KV_W), lambda b, i: (b, 0, 0)),
            pl.BlockSpec((None, seq, SWA_KV_W), lambda b, i: (b, 0, 0)),
        ],
        out_specs=pl.BlockSpec((None, tq, SWA_Q_W), lambda b, i: (b, i, 0)),
        out_shape=jax.ShapeDtypeStruct((bsz, seq, SWA_Q_W), BF16), name="swa",
        compiler_params=_params(("parallel", "parallel")),
    )(sinks, qa, ka, va)


def _diff_body(lam_ref, g_ref, q_ref, k_ref, v_ref, o_ref):
    tq = q_ref.shape[1]
    tk = v_ref.shape[2]
    qi = pl.program_id(2)
    lp = lam_ref[...]
    lam = (jnp.exp(jnp.sum(lp[0:1] * lp[1:2], axis=1, keepdims=True))
           - jnp.exp(jnp.sum(lp[2:3] * lp[3:4], axis=1, keepdims=True)) + LAM_INIT)
    qt = q_ref[...]
    comp = lax.broadcasted_iota(I32, qt.shape, 0) < HEAD_DIM
    zero = jnp.zeros_like(qt)
    qpad = jnp.concatenate([jnp.where(comp, qt, zero), jnp.where(comp, zero, qt)], axis=1)

    def step(j, carry, masked):
        m, l, acc = carry
        ks = pl.multiple_of(j * tk, tk)
        kt = k_ref[pl.ds(ks, tk), :]
        st = jnp.dot(kt, qpad, preferred_element_type=F32)
        if masked:
            kpos = ks + lax.broadcasted_iota(I32, st.shape, 0)
            qcol = lax.broadcasted_iota(I32, st.shape, 1)
            qpos = qi * tq + jnp.where(qcol >= tq, qcol - tq, qcol)
            st = jnp.where(kpos <= qpos, st, NEG_BIG)
        m_new = jnp.maximum(m, jnp.max(st, axis=0, keepdims=True))
        p = jnp.exp(st - m_new)
        alpha = jnp.exp(m - m_new)
        l = alpha * l + jnp.sum(p, axis=0, keepdims=True)
        acc = alpha * acc + jnp.dot(v_ref[j], p.astype(BF16), preferred_element_type=F32)
        return m_new, l, acc

    init = (jnp.full((1, 2 * tq), NEG_BIG, F32), jnp.zeros((1, 2 * tq), F32),
            jnp.zeros((LANES, 2 * tq), F32))
    carry = lax.fori_loop(0, qi, lambda j, c: step(j, c, False), init)
    _, l, acc = step(qi, carry, True)
    inv = 1.0 / l
    ot = acc[:, :tq] * inv[:, :tq] - lam * (acc[:, tq:] * inv[:, tq:])
    ot = ot * lax.rsqrt(jnp.mean(ot * ot, axis=0, keepdims=True) + RMS_EPS)
    o_ref[...] = (ot.T * g_ref[...] * (1.0 - LAM_INIT)).astype(BF16)


def _diff_attn(lam_params, subln_g, qbt, kb, vbt):
    bsz, heads, nq, _, tq = qbt.shape
    seq = kb.shape[1]
    nk, tk = vbt.shape[2], vbt.shape[4]
    assert tq == tk, "the causal split assumes equal query and key tiles"
    grid = (bsz, heads, nq)
    return pl.pallas_call(
        _diff_body, grid=grid,
        in_specs=[
            pl.BlockSpec((4, HEAD_DIM), lambda b, h, i: (0, 0)),
            pl.BlockSpec((1, LANES), lambda b, h, i: (0, 0)),
            pl.BlockSpec((None, None, None, LANES, tq), lambda b, h, i: (b, h, i, 0, 0)),
            pl.BlockSpec((None, seq, LANES), lambda b, h, i: (b, 0, h)),
            pl.BlockSpec((None, None, nk, LANES, tk), lambda b, h, i: (b, h, 0, 0, 0)),
        ],
        out_specs=pl.BlockSpec((None, tq, LANES), lambda b, h, i: (b, i, h)),
        out_shape=jax.ShapeDtypeStruct((bsz, seq, DIFF_W), BF16), name="diff_attn",
        compiler_params=_params(("parallel", "parallel", "parallel")),
    )(lam_params, subln_g, qbt, kb, vbt)


def _rows8(vals, tm):
    rid = lax.broadcasted_iota(I32, (8, tm), 0)
    out = jnp.zeros((8, tm), vals[0].dtype)
    for k, v in enumerate(vals):
        out = jnp.where(rid == k, v, out)
    return out


def _mix_route_body(oa_ref, ob_ref, ga_ref, gb_ref, x_ref, wsa_ref, wdo_ref, wo_ref, g2_ref,
                    rwt_ref, rb_ref,
                    x1_ref, h2_ref, idx_ref, pos_ref, gt_ref, cnt_ref, carry_ref):
    tm = x_ref.shape[0]
    n_exp = rwt_ref.shape[0]

    @pl.when(pl.program_id(0) == 0)
    def _():
        carry_ref[...] = jnp.zeros_like(carry_ref)

    ya = jnp.dot(oa_ref[...], wsa_ref[...], preferred_element_type=F32)
    yb = jnp.dot(ob_ref[...], wdo_ref[...], preferred_element_type=F32)
    mixed = ga_ref[...].astype(F32) * ya + gb_ref[...].astype(F32) * yb
    x1 = x_ref[...] + jnp.dot(mixed.astype(BF16), wo_ref[...], preferred_element_type=F32)
    x1_ref[...] = x1
    ms = jnp.mean(x1 * x1, axis=-1, keepdims=True)
    h2 = x1 * lax.rsqrt(ms + RMS_EPS) * g2_ref[...]
    h2_ref[...] = h2
    lt = lax.dot_general(rwt_ref[...], h2.astype(BF16), (((1,), (1,)), ((), ())),
                         preferred_element_type=F32) + rb_ref[...]
    eio = lax.broadcasted_iota(I32, (n_exp, tm), 0)
    vals, idxs, sels = [], [], []
    cur = lt
    for _ in range(TOP_K):
        m = jnp.max(cur, axis=0, keepdims=True)
        idx = jnp.min(jnp.where(cur == m, eio, n_exp), axis=0, keepdims=True)
        sel = eio == idx
        vals.append(m)
        idxs.append(idx)
        sels.append(sel)
        cur = jnp.where(sel, -jnp.inf, cur)
    exps = [jnp.exp(v - vals[0]) for v in vals]
    inv = 1.0 / (exps[0] + exps[1] + exps[2] + exps[3])
    gates = [e * inv for e in exps]
    selm = jnp.zeros((n_exp, tm), F32)
    for sel in sels:
        selm = selm + sel.astype(F32)
    r = lax.broadcasted_iota(I32, (tm, tm), 0)
    c = lax.broadcasted_iota(I32, (tm, tm), 1)
    upper = (r < c).astype(BF16)
    rank = carry_ref[...] + jnp.dot(selm.astype(BF16), upper, preferred_element_type=F32)
    poss = [jnp.sum(jnp.where(sel, rank, 0.0), axis=0, keepdims=True).astype(I32)
            for sel in sels]
    carry_ref[...] = carry_ref[...] + jnp.sum(selm, axis=1, keepdims=True)
    cnt_ref[...] = jnp.broadcast_to(carry_ref[...], cnt_ref.shape)
    idx_ref[...] = _rows8(idxs, tm)
    pos_ref[...] = _rows8(poss, tm)
    g8 = _rows8(gates, tm)
    gpad = jnp.concatenate([g8, jnp.zeros((LANES - 8, tm), F32)], axis=0)
    for cc in range(tm // LANES):
        gt_ref[cc * LANES:(cc + 1) * LANES, :] = gpad[:, cc * LANES:(cc + 1) * LANES].T


def _mix_route(oa, ob, ga, gb, x, wsa, wdo, wo, g2, rwt, rb):
    n, d = x.shape
    tm = min(TM_MIX, n)
    n_exp = rwt.shape[0]
    grid = (n // tm,)
    row = lambda i: (i, 0)
    const = lambda i: (0, 0)
    col = lambda i: (0, i)
    in_specs = [
        pl.BlockSpec((tm, SWA_Q_W), row), pl.BlockSpec((tm, DIFF_W), row),
        pl.BlockSpec((tm, d), row), pl.BlockSpec((tm, d), row), pl.BlockSpec((tm, d), row),
        pl.BlockSpec((SWA_Q_W, d), const), pl.BlockSpec((DIFF_W, d), const),
        pl.BlockSpec((d, d), const), pl.BlockSpec((1, d), const),
        pl.BlockSpec((n_exp, d), const), pl.BlockSpec((n_exp, 1), const),
    ]
    out_shape = (
        jax.ShapeDtypeStruct((n, d), F32), jax.ShapeDtypeStruct((n, d), F32),
        jax.ShapeDtypeStruct((8, n), I32), jax.ShapeDtypeStruct((8, n), I32),
        jax.ShapeDtypeStruct((n, LANES), F32), jax.ShapeDtypeStruct((n_exp, LANES), F32),
    )
    out_specs = (
        pl.BlockSpec((tm, d), row), pl.BlockSpec((tm, d), row),
        pl.BlockSpec((8, tm), col), pl.BlockSpec((8, tm), col),
        pl.BlockSpec((tm, LANES), row), pl.BlockSpec((n_exp, LANES), const),
    )
    return pl.pallas_call(
        _mix_route_body, grid=grid, in_specs=in_specs, out_specs=out_specs,
        out_shape=out_shape, scratch_shapes=[pltpu.VMEM((n_exp, 1), F32)], name="mix_route",
        compiler_params=_params(("arbitrary",)),
    )(oa, ob, ga, gb, x, wsa, wdo, wo, g2, rwt, rb)


def _row_copy(src, dst, sem):
    return pltpu.make_async_copy(src, dst, sem)


def _dispatch_body(dest_ref, h_ref, xs_ref, sem):
    tm = h_ref.shape[0]

    def issue(r, _):
        for k in range(TOP_K):
            d = dest_ref[k, r]
            _row_copy(h_ref.at[pl.ds(r, 1), :], xs_ref.at[pl.ds(d, 1), :], sem).start()
        return 0

    lax.fori_loop(0, tm, issue, 0)

    def drain(r, _):
        for k in range(TOP_K):
            _row_copy(h_ref.at[pl.ds(0, 1), :], xs_ref.at[pl.ds(0, 1), :], sem).wait()
        return 0

    lax.fori_loop(0, tm, drain, 0)


def _dispatch(dest, h2):
    n, d = h2.shape
    tm = min(TM_DISPATCH, n)
    return pl.pallas_call(
        _dispatch_body, grid=(n // tm,),
        in_specs=[
            pl.BlockSpec((TOP_K, tm), lambda i: (0, i), memory_space=pltpu.SMEM),
            pl.BlockSpec((tm, d), lambda i: (i, 0)),
        ],
        out_specs=pl.BlockSpec(memory_space=pl.ANY),
        out_shape=jax.ShapeDtypeStruct((n * TOP_K, d), F32),
        scratch_shapes=[pltpu.SemaphoreType.DMA(())], name="dispatch",
        compiler_params=_params(("arbitrary",)),
    )(dest, h2)


def _ffn_body(blk_ref, exp_ref, lo_ref, hi_ref, x_ref, w1_ref, b1_ref, w2_ref, b2_ref, y_ref):
    t = pl.program_id(0)
    lo = lo_ref[t]
    hi = hi_ref[t]
    first = jnp.logical_or(t == 0, blk_ref[t] != blk_ref[jnp.maximum(t - 1, 0)])

    @pl.when(hi > lo)
    def _():
        d_ff = w2_ref.shape[0]
        h = jnp.dot(x_ref[...].astype(BF16), w1_ref[...], preferred_element_type=F32) + b1_ref[...]
        g = jnp.minimum(h[:, :d_ff], SWIGLU_LIMIT)
        u = jnp.clip(h[:, d_ff:], -SWIGLU_LIMIT, SWIGLU_LIMIT)
        act = g * jax.nn.sigmoid(SWIGLU_ALPHA * g) * (u + 1.0)
        y = jnp.dot(act.astype(BF16), w2_ref[...], preferred_element_type=F32) + b2_ref[...]
        rows = lax.broadcasted_iota(I32, y.shape, 0)
        mine = (rows >= lo) & (rows < hi)

        @pl.when(first)
        def _():
            y_ref[...] = jnp.where(mine, y, 0.0)

        @pl.when(jnp.logical_not(first))
        def _():
            y_ref[...] = jnp.where(mine, y, y_ref[...])


def _moe_ffn(item_blk, item_exp, item_lo, item_hi, xs, w1, b1, w2, b2):
    rows, d = xs.shape
    n_exp, _, two_f = w1.shape
    d_ff = two_f // 2
    n_items = item_blk.shape[0]
    blk = BLK_FFN
    grid_spec = pltpu.PrefetchScalarGridSpec(
        num_scalar_prefetch=4, grid=(n_items,),
        in_specs=[
            pl.BlockSpec((blk, d), lambda t, b, e, lo, hi: (b[t], 0)),
            pl.BlockSpec((None, d, two_f), lambda t, b, e, lo, hi: (e[t], 0, 0)),
            pl.BlockSpec((None, 1, two_f), lambda t, b, e, lo, hi: (e[t], 0, 0)),
            pl.BlockSpec((None, d_ff, d), lambda t, b, e, lo, hi: (e[t], 0, 0)),
            pl.BlockSpec((None, 1, d), lambda t, b, e, lo, hi: (e[t], 0, 0)),
        ],
        out_specs=pl.BlockSpec((blk, d), lambda t, b, e, lo, hi: (b[t], 0)),
    )
    return pl.pallas_call(
        _ffn_body, grid_spec=grid_spec,
        out_shape=jax.ShapeDtypeStruct((rows, d), F32), name="moe_ffn",
        compiler_params=_params(("arbitrary",)),
    )(item_blk, item_exp, item_lo, item_hi, xs, w1, b1, w2, b2)


def _combine_body(dcur_ref, dnxt_ref, gt_ref, x1_ref, gf_ref, ys_ref, o_ref, buf, sems):
    tm = x1_ref.shape[0]
    i = pl.program_id(0)
    n = pl.num_programs(0)
    slot = lax.rem(i, 2)

    def issue(dref, s):
        def body(r, _):
            for k in range(TOP_K):
                d = dref[k, r]
                _row_copy(ys_ref.at[pl.ds(d, 1), :], buf.at[s, k, pl.ds(r, 1), :],
                          sems.at[s]).start()
            return 0
        lax.fori_loop(0, tm, body, 0)

    @pl.when(i == 0)
    def _():
        issue(dcur_ref, 0)

    @pl.when(i + 1 < n)
    def _():
        issue(dnxt_ref, 1 - slot)

    def drain(r, _):
        for k in range(TOP_K):
            _row_copy(ys_ref.at[pl.ds(0, 1), :], buf.at[slot, 0, pl.ds(0, 1), :],
                      sems.at[slot]).wait()
        return 0

    lax.fori_loop(0, tm, drain, 0)
    gt = gt_ref[...]
    acc = x1_ref[...]
    for k in range(TOP_K):
        acc = acc + gt[:, k:k + 1] * buf[slot, k]
    ms = jnp.mean(acc * acc, axis=-1, keepdims=True)
    o_ref[...] = acc * lax.rsqrt(ms + RMS_EPS) * gf_ref[...]


def _combine(dest, gt, x1, gf, ys):
    n, d = x1.shape
    tm = min(TM_COMBINE, n)
    steps = n // tm
    return pl.pallas_call(
        _combine_body, grid=(steps,),
        in_specs=[
            pl.BlockSpec((TOP_K, tm), lambda i: (0, i), memory_space=pltpu.SMEM),
            pl.BlockSpec((TOP_K, tm), lambda i: (0, jnp.minimum(i + 1, steps - 1)),
                         memory_space=pltpu.SMEM),
            pl.BlockSpec((tm, LANES), lambda i: (i, 0)),
            pl.BlockSpec((tm, d), lambda i: (i, 0)),
            pl.BlockSpec((1, d), lambda i: (0, 0)),
            pl.BlockSpec(memory_space=pl.ANY),
        ],
        out_specs=pl.BlockSpec((tm, d), lambda i: (i, 0)),
        out_shape=jax.ShapeDtypeStruct((n, d), F32),
        scratch_shapes=[pltpu.VMEM((2, TOP_K, tm, d), F32), pltpu.SemaphoreType.DMA((2,))],
        name="combine", compiler_params=_params(("arbitrary",)),
    )(dest, dest, gt, x1, gf, ys)


def _rope_tables(seq):
    pos = jnp.arange(seq, dtype=F32)
    inv_freq = ROPE_THETA ** (-jnp.arange(0, HEAD_DIM, 2, dtype=F32) / HEAD_DIM)
    ang = pos[:, None] * inv_freq[None, :]
    cos, sin = jnp.cos(ang), jnp.sin(ang)
    reps = LANES // HEAD_DIM
    cos_t = jnp.tile(jnp.concatenate([cos, cos], axis=1), (1, reps))
    sin_t = jnp.tile(jnp.concatenate([-sin, sin], axis=1), (1, reps))
    return cos_t, sin_t


def _work_items(counts, n_rows):
    blk = BLK_FFN
    n_items = n_rows // blk + N_EXPERTS - 1
    grp_end = jnp.cumsum(counts)
    grp_start = grp_end - counts
    first_blk = grp_start // blk
    last_blk = (grp_end - 1) // blk
    per_exp = jnp.where(counts > 0, last_blk - first_blk + 1, 0)
    item_end = jnp.cumsum(per_exp)
    item_start = item_end - per_exp
    total = item_end[-1]
    t = jnp.arange(n_items, dtype=I32)
    tc = jnp.minimum(t, total - 1)
    e_t = jnp.minimum(jnp.searchsorted(item_end, tc, side="right"), N_EXPERTS - 1).astype(I32)
    blk_t = first_blk[e_t] + (tc - item_start[e_t])
    lo = jnp.maximum(grp_start[e_t], blk_t * blk) - blk_t * blk
    hi = jnp.minimum(grp_end[e_t], (blk_t + 1) * blk) - blk_t * blk
    active = t < total
    lo = jnp.where(active, lo, 0)
    hi = jnp.where(active, hi, 0)
    return blk_t.astype(I32), e_t, lo.astype(I32), hi.astype(I32), grp_start


def kernel(x, attn_norm_g, w_in, b_in, swa_sinks, diff_lambda_q1, diff_lambda_k1, diff_lambda_q2,
           diff_lambda_k2, diff_subln_g, w_swa_out, w_diff_out, w_o, ffn_norm_g, router_w,
           router_b, w1, b1, w2, b2, final_norm_g):
    bsz, seq, d = x.shape
    n = bsz * seq
    layer = 0
    cos_t, sin_t = _rope_tables(seq)
    qa, ka, va, qbt, kb, vbt, ga, gb = _in_proj(
        x, attn_norm_g[layer][None, :], w_in[layer].astype(BF16), b_in[layer][None, :],
        cos_t, sin_t)
    oa = _swa(swa_sinks[layer], qa, ka, va)
    lam_params = jnp.stack([diff_lambda_q1[layer], diff_lambda_k1[layer],
                            diff_lambda_q2[layer], diff_lambda_k2[layer]])
    ob = _diff_attn(lam_params, diff_subln_g[layer][None, :], qbt, kb, vbt)
    x1, h2, idx8, pos8, gt, cnt = _mix_route(
        oa.reshape(n, SWA_Q_W), ob.reshape(n, DIFF_W), ga.reshape(n, d), gb.reshape(n, d),
        x.reshape(n, d), w_swa_out[layer].astype(BF16), w_diff_out[layer].astype(BF16),
        w_o[layer].astype(BF16), ffn_norm_g[layer][None, :],
        router_w[layer].T.astype(BF16), router_b[layer][:, None])
    counts = cnt[:, 0].astype(I32)
    item_blk, item_exp, item_lo, item_hi, grp_start = _work_items(counts, n * TOP_K)
    dest = grp_start[idx8[:TOP_K]] + pos8[:TOP_K]
    xs = _dispatch(dest, h2)
    ys = _moe_ffn(item_blk, item_exp, item_lo, item_hi, xs,
                  w1[layer].astype(BF16), b1[layer][:, None, :],
                  w2[layer].astype(BF16), b2[layer][:, None, :])
    out = _combine(dest, gt, x1, final_norm_g[None, :], ys)
    return out.reshape(bsz, seq, d)
```
